```python
import jax, jax.numpy as jnp
from jax import lax
import numpy as np

D_MODEL = 1024
BATCH = 2
SEQ = 8192
DEPTH = 4

N_A = DEPTH // 2
N_B = DEPTH - N_A
POOL_WINDOWS = (2, 4, 8, 16)
POOL_GROUPS = len(POOL_WINDOWS)
GC = D_MODEL // POOL_GROUPS
HEAD_DIM = 64
N_HEADS = D_MODEL // HEAD_DIM
N_KV = max(1, N_HEADS // 8)
GROUP = N_HEADS // N_KV
WINDOW = 128
BLK = WINDOW
D_FF = 2816
CONV_W = 3
EPS = 1e-5

kernel_name = "yoco_pool_swa_sink_convffn"


def rmsnorm(x, g):
    xf = x.astype(jnp.float32)
    y = xf * lax.rsqrt(jnp.mean(xf * xf, axis=-1, keepdims=True) + EPS)
    return (y * g.astype(jnp.float32)).astype(x.dtype)


def pool_mixer(h, w, scale):
    B, S, D = h.shape
    c = jnp.cumsum(h.astype(jnp.float32), axis=1)
    t = jnp.arange(S)
    pooled = []
    for gi, win in enumerate(POOL_WINDOWS):
        cg = c[..., gi * GC:(gi + 1) * GC]
        lag = jnp.pad(cg, ((0, 0), (win, 0), (0, 0)))[:, :S]
        cnt = jnp.minimum(t + 1, win).astype(jnp.float32)[None, :, None]
        pooled.append((cg - lag) / cnt)
    pooled = jnp.stack(pooled, axis=2).astype(h.dtype) - h.reshape(B, S, POOL_GROUPS, GC)
    y = jnp.einsum('bsgc,gcd->bsgd', pooled, w).reshape(B, S, D)
    return y * scale


def conv_ffn(h, w_up, conv_w, conv_b, w_down):
    u = h @ w_up
    S = u.shape[1]
    up = jnp.pad(u, ((0, 0), (CONV_W - 1, 0), (0, 0)))
    u = sum(conv_w[k] * up[:, k:k + S] for k in range(CONV_W)) + conv_b
    gate, val = jnp.split(u, 2, axis=-1)
    return (jax.nn.silu(gate) * val) @ w_down


def swa_sink_attention(q, k, v, sinks):
    B, S = q.shape[:2]
    nb = S // BLK
    qb = q.reshape(B, nb, BLK, N_KV, GROUP, HEAD_DIM)
    kb = k.reshape(B, nb, BLK, N_KV, HEAD_DIM)
    vb = v.reshape(B, nb, BLK, N_KV, HEAD_DIM)
    pad = ((0, 0), (1, 0), (0, 0), (0, 0), (0, 0))
    kw = jnp.concatenate([jnp.pad(kb, pad)[:, :nb], kb], axis=2)
    vw = jnp.concatenate([jnp.pad(vb, pad)[:, :nb], vb], axis=2)
    s = jnp.einsum('bnqkgd,bnskd->bnkgqs', qb, kw).astype(jnp.float32) * (HEAD_DIM ** -0.5)
    qi = jnp.arange(BLK)[:, None]
    si = jnp.arange(2 * BLK)[None, :]
    band = (si > qi) & (si <= qi + BLK)
    valid = (jnp.arange(nb)[:, None, None] > 0) | (si >= BLK)[None]
    mask = (band[None] & valid)[None, :, None, None]
    sink = sinks.astype(jnp.float32).reshape(N_KV, GROUP)[None, None, :, :, None, None]
    s = jnp.where(mask, s, -jnp.inf)
    m = jnp.maximum(jnp.max(s, axis=-1, keepdims=True), sink)
    p = jnp.exp(s - m)
    denom = jnp.sum(p, axis=-1, keepdims=True) + jnp.exp(sink - m)
    pr = (p / denom).astype(v.dtype)
    o = jnp.einsum('bnkgqs,bnskd->bnqkgd', pr, vw)
    return o.reshape(B, S, N_HEADS * HEAD_DIM)


def setup_inputs(seed: int = 0) -> dict:
    key = jax.random.key(seed)
    ks = jax.random.split(key, 20)
    f32 = jnp.float32
    nrm = lambda k, shp, s: jax.random.normal(k, shp, f32) * s
    KVW = 2 * N_KV * HEAD_DIM
    QW = N_HEADS * HEAD_DIM
    return {
        "x": nrm(ks[0], (BATCH, SEQ, D_MODEL), 1.0),
        "norm1_g": 1.0 + nrm(ks[1], (DEPTH, D_MODEL), 0.02),
        "norm2_g": 1.0 + nrm(ks[2], (DEPTH, D_MODEL), 0.02),
        "pool_w": nrm(ks[3], (N_A, POOL_GROUPS, GC, GC), GC ** -0.5),
        "pool_scale": 1.0 + nrm(ks[4], (N_A, D_MODEL), 0.02),
        "kv_norm_g": 1.0 + nrm(ks[5], (D_MODEL,), 0.02),
        "w_kv": nrm(ks[6], (D_MODEL, KVW), D_MODEL ** -0.5),
        "b_kv": nrm(ks[7], (KVW,), 0.02),
        "w_q": nrm(ks[8], (N_B, D_MODEL, QW), D_MODEL ** -0.5),
        "b_q": nrm(ks[9], (N_B, QW), 0.02),
        "sinks": nrm(ks[10], (N_B, N_HEADS), 1.0),
        "w_o": nrm(ks[11], (N_B, QW, D_MODEL), QW ** -0.5),
        "b_o": nrm(ks[12], (N_B, D_MODEL), 0.02),
        "ffn_up": nrm(ks[13], (DEPTH, D_MODEL, 2 * D_FF), D_MODEL ** -0.5),
        "ffn_conv_w": nrm(ks[14], (DEPTH, CONV_W, 2 * D_FF), CONV_W ** -0.5),
        "ffn_conv_b": nrm(ks[15], (DEPTH, 2 * D_FF), 0.02),
        "ffn_down": nrm(ks[16], (DEPTH, D_FF, D_MODEL), D_FF ** -0.5),
        "final_g": 1.0 + nrm(ks[17], (D_MODEL,), 0.02),
    }


def reference(x, norm1_g, norm2_g, pool_w, pool_scale, kv_norm_g, w_kv, b_kv, w_q, b_q,
              sinks, w_o, b_o, ffn_up, ffn_conv_w, ffn_conv_b, ffn_down, final_g):
    B, S, D = x.shape
    k_sh = v_sh = None
    for l in range(DEPTH):
        h = rmsnorm(x, norm1_g[l])
        if l < N_A:
            x = x + pool_mixer(h, pool_w[l], pool_scale[l])
        else:
            j = l - N_A
            q = (h @ w_q[j] + b_q[j]).reshape(B, S, N_HEADS, HEAD_DIM)
            o = swa_sink_attention(q, k_sh, v_sh, sinks[j])
            x = x + (o @ w_o[j] + b_o[j])
        h = rmsnorm(x, norm2_g[l])
        x = x + conv_ffn(h, ffn_up[l], ffn_conv_w[l], ffn_conv_b[l], ffn_down[l])
        if l == N_A - 1:
            kv = rmsnorm(x, kv_norm_g) @ w_kv + b_kv
            k_sh, v_sh = jnp.split(kv.reshape(B, S, 2 * N_KV, HEAD_DIM), 2, axis=2)
    return rmsnorm(x, final_g)
```

```python
import functools

import jax
import jax.numpy as jnp
import numpy as np
from jax import lax
from jax.experimental import pallas as pl
from jax.experimental.pallas import tpu as pltpu

F32 = jnp.float32
BF16 = jnp.bfloat16

POOL_WINDOWS = (2, 4, 8, 16)
MAX_POOL_WINDOW = max(POOL_WINDOWS)
HEAD_DIM = 64
N_KV = 2
WINDOW = 128
CONV_W = 3
EPS = 1e-5
NEG_BIG = -1e30

F32_SUBLANES = 8
LANES = 128
VMEM_LIMIT_BYTES = 56 * 1024 * 1024

TOKEN_TILE = 512
FF_CHUNK = 512


def _ff_chunks(d_ff):
    chunks, off = [], 0
    while off < d_ff:
        ck = min(FF_CHUNK, d_ff - off)
        chunks.append((off, ck))
        off += ck
    return tuple(chunks)


def _rmsnorm(x, g):
    ms = jnp.mean(x * x, axis=-1, keepdims=True)
    return x * lax.rsqrt(ms + EPS) * g


def _dot(a, b):
    return jnp.dot(a, b, preferred_element_type=F32)


def _conv_ffn(h2, wup_ref, cw_ref, cb_ref, wdn_ref, ubuf_ref, tail_ref, tm, d_ff):
    halo = F32_SUBLANES
    acc = None
    for c, (off, ck) in enumerate(_ff_chunks(d_ff)):
        slot = c % 2
        conv = []
        for part in range(2):
            lo = part * d_ff + off
            u = _dot(h2, wup_ref[:, lo:lo + ck])
            ubuf_ref[slot, part, 0:halo, 0:ck] = tail_ref[:, lo:lo + ck]
            ubuf_ref[slot, part, halo:halo + tm, 0:ck] = u
            tail_ref[:, lo:lo + ck] = u[tm - halo:tm]
            w = cw_ref[:, lo:lo + ck]
            u_m2 = ubuf_ref[slot, part, halo - 2:halo - 2 + tm, 0:ck]
            u_m1 = ubuf_ref[slot, part, halo - 1:halo - 1 + tm, 0:ck]
            conv.append(u_m2 * w[0:1] + u_m1 * w[1:2] + u * w[2:3] + cb_ref[:, lo:lo + ck])
        gate, val = conv
        act = (gate / (1.0 + jnp.exp(-gate)) * val).astype(BF16)
        d = _dot(act, wdn_ref[off:off + ck, :])
        acc = d if acc is None else acc + d
    return acc


def _pool_mixer(h1, hh_ref, poolw_ref, tpos, tm):
    halo = MAX_POOL_WINDOW
    gc = h1.shape[1] // len(POOL_WINDOWS)
    hh_ref[halo:halo + tm, :] = h1
    ys = []
    for gi, win in enumerate(POOL_WINDOWS):
        cols = slice(gi * gc, (gi + 1) * gc)
        hg = h1[:, cols]
        wsum = hg
        for j in range(1, win):
            wsum = wsum + hh_ref[halo - j:halo - j + tm, cols]
        inv_cnt = 1.0 / jnp.minimum(tpos + 1, win).astype(F32)
        pooled = wsum * inv_cnt - hg
        ys.append(_dot(pooled.astype(BF16), poolw_ref[gi]))
    hh_ref[0:halo, :] = hh_ref[tm:tm + halo, :]
    return jnp.concatenate(ys, axis=1)


def _swa_attention(q, kvc_ref, kvp_ref, sinks_ref, o_ref, is_first, tm):
    n_pairs = q.shape[1] // LANES
    kdim = N_KV * HEAD_DIM
    lane = lax.broadcasted_iota(jnp.int32, (1, LANES), 1)
    first_half = lane < HEAD_DIM
    qi = lax.broadcasted_iota(jnp.int32, (WINDOW, 2 * WINDOW), 0)
    si = lax.broadcasted_iota(jnp.int32, (WINDOW, 2 * WINDOW), 1)
    band = (si > qi) & (si <= qi + WINDOW)
    band = jnp.concatenate([band] * n_pairs, axis=0)
    si_all = jnp.concatenate([si] * n_pairs, axis=0)
    first_key = jnp.where(is_first, WINDOW, 0)
    sink_a = jnp.concatenate(
        [jnp.full((WINDOW, 1), sinks_ref[j], F32) for j in range(n_pairs)], axis=0)
    sink_b = jnp.concatenate(
        [jnp.full((WINDOW, 1), sinks_ref[n_pairs + j], F32) for j in range(n_pairs)], axis=0)
    zero = jnp.zeros((), BF16)
    contract_last = (((1,), (1,)), ((), ()))
    for nb in range(tm // WINDOW):
        rows = slice(nb * WINDOW, (nb + 1) * WINDOW)
        if nb == 0:
            prev = kvp_ref[...]
            mask = band & (si_all >= first_key)
        else:
            prev = kvc_ref[(nb - 1) * WINDOW:nb * WINDOW, :]
            mask = band
        kvw = jnp.concatenate([prev, kvc_ref[rows, :]], axis=0)
        kw, vw = kvw[:, 0:kdim], kvw[:, kdim:2 * kdim]
        k_a, k_b = jnp.where(first_half, kw, zero), jnp.where(first_half, zero, kw)
        v_a, v_b = jnp.where(first_half, vw, zero), jnp.where(first_half, zero, vw)
        qst = jnp.concatenate(
            [q[rows, j * LANES:(j + 1) * LANES] for j in range(n_pairs)], axis=0)
        stats = []
        for k_x, sink in ((k_a, sink_a), (k_b, sink_b)):
            s = lax.dot_general(qst, k_x, contract_last, preferred_element_type=F32)
            s = jnp.where(mask, s, NEG_BIG)
            m = jnp.maximum(jnp.max(s, axis=-1, keepdims=True), sink)
            p = jnp.exp(s - m)
            denom = jnp.sum(p, axis=-1, keepdims=True) + jnp.exp(sink - m)
            stats.append((p.astype(BF16), 1.0 / denom))
        (p_a, r_a), (p_b, r_b) = stats
        o_st = _dot(p_a, v_a) + _dot(p_b, v_b)
        o_st = (o_st * jnp.where(first_half, r_a, r_b)).astype(BF16)
        for j in range(n_pairs):
            o_ref[rows, j * LANES:(j + 1) * LANES] = o_st[j * WINDOW:(j + 1) * WINDOW]


def _layer_kernel(*refs, mixer, emit_kv, final_norm, tm, d_ff, tiles_per_seq):
    refs = list(refs)
    x_ref, g1_ref, g2_ref = refs[0:3]
    pos = 3
    if mixer == "pool":
        poolw_ref, pscale_ref = refs[pos:pos + 2]
        pos += 2
    else:
        kvc_ref, kvp_ref, wq_ref, bq_ref, sinks_ref, wo_ref, bo_ref = refs[pos:pos + 7]
        pos += 7
    wup_ref, cw_ref, cb_ref, wdn_ref = refs[pos:pos + 4]
    pos += 4
    if emit_kv:
        kvg_ref, wkv_ref, bkv_ref = refs[pos:pos + 3]
        pos += 3
    if final_norm:
        fg_ref = refs[pos]
        pos += 1
    xo_ref = refs[pos]
    pos += 1
    if emit_kv:
        kvo_ref = refs[pos]
        pos += 1
    ubuf_ref, tail_ref, mix_ref = refs[pos:pos + 3]

    i = pl.program_id(0)
    seq_tile = i % tiles_per_seq
    is_first = seq_tile == 0

    @pl.when(is_first)
    def _():
        tail_ref[...] = jnp.zeros_like(tail_ref)
        if mixer == "pool":
            mix_ref[0:MAX_POOL_WINDOW, :] = jnp.zeros((MAX_POOL_WINDOW, mix_ref.shape[1]), F32)

    x = x_ref[...]
    h1 = _rmsnorm(x, g1_ref[...])
    if mixer == "pool":
        tpos = seq_tile * tm + lax.broadcasted_iota(jnp.int32, (tm, 1), 0)
        y = _pool_mixer(h1, mix_ref, poolw_ref, tpos, tm)
        x = x + y * pscale_ref[...]
    else:
        q = (_dot(h1.astype(BF16), wq_ref[...]) + bq_ref[...]) * (HEAD_DIM ** -0.5)
        _swa_attention(q.astype(BF16), kvc_ref, kvp_ref, sinks_ref, mix_ref, is_first, tm)
        x = x + _dot(mix_ref[...], wo_ref[...]) + bo_ref[...]

    h2 = _rmsnorm(x, g2_ref[...]).astype(BF16)
    x = x + _conv_ffn(h2, wup_ref, cw_ref, cb_ref, wdn_ref, ubuf_ref, tail_ref, tm, d_ff)

    if emit_kv:
        kvn = _rmsnorm(x, kvg_ref[...]).astype(BF16)
        kvo_ref[...] = (_dot(kvn, wkv_ref[...]) + bkv_ref[...]).astype(BF16)
    if final_norm:
        x = _rmsnorm(x, fg_ref[...])
    xo_ref[...] = x


def _resident(shape, layer=None):
    if layer is None:
        return pl.BlockSpec(shape, lambda i: (0,) * len(shape), pipeline_mode=pl.Buffered(1))
    return pl.BlockSpec((None,) + shape, lambda i: (layer,) + (0,) * len(shape),
                        pipeline_mode=pl.Buffered(1))


def _run_layer(x, layer, p, *, mixer, seq_len, kv=None, emit_kv=False, final_norm=False):
    n_tok, d = x.shape
    d_ff = p["ffn_down"].shape[1]
    tm = TOKEN_TILE
    assert seq_len % tm == 0 and n_tok % seq_len == 0 and tm % WINDOW == 0
    row = lambda n: _resident((1, n), layer)
    tok = lambda n: pl.BlockSpec((tm, n), lambda i: (i, 0))

    args = [x, p["norm1_g"], p["norm2_g"]]
    specs = [tok(d), row(d), row(d)]
    if mixer == "pool":
        gc = d // len(POOL_WINDOWS)
        args += [p["pool_w"], p["pool_scale"]]
        specs += [_resident((len(POOL_WINDOWS), gc, gc), layer), row(d)]
        mix_scratch = pltpu.VMEM((tm + MAX_POOL_WINDOW, d), F32)
    else:
        j = layer - p["n_pool"]
        blocks_per_tile = tm // WINDOW
        kvw = kv.shape[1]
        prev_spec = pl.BlockSpec(
            (WINDOW, kvw), lambda i: (jnp.maximum(i * blocks_per_tile - 1, 0), 0))
        args += [kv, kv, p["w_q"], p["b_q"], p["sinks"][j], p["w_o"], p["b_o"]]
        specs += [tok(kvw), prev_spec, _resident((d, d), j), _resident((1, d), j),
                  pl.BlockSpec(memory_space=pltpu.SMEM),
                  _resident((d, d), j), _resident((1, d), j)]
        mix_scratch = pltpu.VMEM((tm, d), BF16)
    args += [p["ffn_up"], p["ffn_conv_w"], p["ffn_conv_b"], p["ffn_down"]]
    specs += [_resident((d, 2 * d_ff), layer), _resident((CONV_W, 2 * d_ff), layer),
              row(2 * d_ff), _resident((d_ff, d), layer)]
    if emit_kv:
        kvw = p["w_kv"].shape[1]
        args += [p["kv_norm_g"], p["w_kv"], p["b_kv"]]
        specs += [_resident((1, d)), _resident((d, kvw)), _resident((1, kvw))]
    if final_norm:
        args += [p["final_g"]]
        specs += [_resident((1, d))]

    out_shape = [jax.ShapeDtypeStruct((n_tok, d), F32)]
    out_specs = [tok(d)]
    if emit_kv:
        out_shape.append(jax.ShapeDtypeStruct((n_tok, kvw), BF16))
        out_specs.append(tok(kvw))

    kern = functools.partial(
        _layer_kernel, mixer=mixer, emit_kv=emit_kv, final_norm=final_norm,
        tm=tm, d_ff=d_ff, tiles_per_seq=seq_len // tm)
    out = pl.pallas_call(
        kern,
        grid=(n_tok // tm,),
        in_specs=specs,
        out_specs=out_specs,
        out_shape=out_shape,
        scratch_shapes=[
            pltpu.VMEM((2, 2, tm + F32_SUBLANES, FF_CHUNK), F32),
            pltpu.VMEM((F32_SUBLANES, 2 * d_ff), F32),
            mix_scratch,
        ],
        compiler_params=pltpu.CompilerParams(
            dimension_semantics=("arbitrary",), vmem_limit_bytes=VMEM_LIMIT_BYTES),
        name=f"layer{layer}_{mixer}",
    )(*args)
    return out


def kernel(x, norm1_g, norm2_g, pool_w, pool_scale, kv_norm_g, w_kv, b_kv, w_q, b_q, sinks,
           w_o, b_o, ffn_up, ffn_conv_w, ffn_conv_b, ffn_down, final_g):
    batch, seq_len, d = x.shape
    depth = norm1_g.shape[0]
    n_pool = pool_w.shape[0]
    n_heads = d // HEAD_DIM
    half = n_heads // 2
    perm = np.concatenate([
        np.concatenate([np.arange(j * HEAD_DIM, (j + 1) * HEAD_DIM),
                        np.arange((half + j) * HEAD_DIM, (half + j + 1) * HEAD_DIM)])
        for j in range(half)])
    p = {
        "n_pool": n_pool,
        "norm1_g": norm1_g.reshape(depth, 1, d), "norm2_g": norm2_g.reshape(depth, 1, d),
        "pool_w": pool_w.astype(BF16), "pool_scale": pool_scale.reshape(n_pool, 1, d),
        "kv_norm_g": kv_norm_g.reshape(1, d), "w_kv": w_kv.astype(BF16),
        "b_kv": b_kv.reshape(1, -1),
        "w_q": w_q[:, :, perm].astype(BF16), "b_q": b_q[:, perm].reshape(-1, 1, d),
        "sinks": sinks, "w_o": w_o[:, perm, :].astype(BF16), "b_o": b_o.reshape(-1, 1, d),
        "ffn_up": ffn_up.astype(BF16), "ffn_conv_w": ffn_conv_w,
        "ffn_conv_b": ffn_conv_b.reshape(depth, 1, -1), "ffn_down": ffn_down.astype(BF16),
        "final_g": final_g.reshape(1, d),
    }
    xt = x.reshape(batch * seq_len, d)
    kv = None
    for layer in range(depth):
        last_pool = layer == n_pool - 1
        if layer < n_pool:
            out = _run_layer(xt, layer, p, mixer="pool", seq_len=seq_len, emit_kv=last_pool)
        else:
            out = _run_layer(xt, layer, p, mixer="attn", seq_len=seq_len, kv=kv,
                             final_norm=layer == depth - 1)
        xt = out[0]
        if last_pool:
            kv = out[1]
    return xt.reshape(batch, seq_len, d)
```

```python
import functools

import jax
import jax.numpy as jnp
import numpy as np
from jax import lax
from jax.experimental import pallas as pl
from jax.experimental.pallas import tpu as pltpu

F32 = jnp.float32
BF16 = jnp.bfloat16

POOL_WINDOWS = (2, 4, 8, 16)
HEAD_DIM = 64
N_KV = 2
WINDOW = 128
CONV_W = 3
EPS = 1e-5
NEG_BIG = -1e30

SUBLANES = 8
LANES = 128
BLOCK = WINDOW
RUN = BLOCK // SUBLANES
VMEM_LIMIT_BYTES = 56 * 1024 * 1024

TOKEN_TILE = 512
FF_CHUNK = 512


def _ff_chunks(d_ff):
    chunks, off = [], 0
    while off < d_ff:
        ck = min(FF_CHUNK, d_ff - off)
        chunks.append((off, ck))
        off += ck
    return tuple(chunks)


def _rmsnorm(x, g):
    ms = jnp.mean(x * x, axis=-1, keepdims=True)
    return x * lax.rsqrt(ms + EPS) * g


def _dot(a, b):
    return jnp.dot(a, b, preferred_element_type=F32)


def _blocks(a):
    return a.reshape(a.shape[0] // BLOCK, RUN, SUBLANES, a.shape[1])


def _unblocks(a5):
    return a5.reshape(a5.shape[0] * BLOCK, a5.shape[3])


def _wrapped(a5, carry, k):
    last = a5[:, RUN - k:RUN]
    prev_last = jnp.concatenate([carry[None], last[:-1]], axis=0)
    sub = lax.broadcasted_iota(jnp.int32, (1, 1, SUBLANES, 1), 2)
    return pltpu.roll(jnp.where(sub == SUBLANES - 1, prev_last, last), 1, axis=2)


def _shifted(a5, halo, j):
    k = halo.shape[1]
    return jnp.concatenate([halo[:, k - j:k], a5[:, 0:RUN - j]], axis=1)


def _load_permuted(x_ref, slab_ref):
    tm, d = x_ref.shape
    for c in range(d // LANES):
        slab_ref[c] = x_ref[:, c * LANES:(c + 1) * LANES]
    groups = []
    for n in range(tm // BLOCK):
        for r in range(RUN):
            groups.append(jnp.concatenate(
                [slab_ref[c, pl.ds(BLOCK * n + r, SUBLANES, stride=RUN), :]
                 for c in range(d // LANES)], axis=1))
    return jnp.concatenate(groups, axis=0)


def _store_unpermuted(y, o_ref, slab_ref):
    tm, d = o_ref.shape
    for n in range(tm // BLOCK):
        for r in range(RUN):
            row0 = (n * RUN + r) * SUBLANES
            for c in range(d // LANES):
                slab_ref[c, pl.ds(BLOCK * n + r, SUBLANES, stride=RUN), :] = (
                    y[row0:row0 + SUBLANES, c * LANES:(c + 1) * LANES])
    for c in range(d // LANES):
        o_ref[:, c * LANES:(c + 1) * LANES] = slab_ref[c]


def _ffn_up(off, ck, part, h2_ref, wup_ref, d_ff):
    lo = part * d_ff + off
    return _dot(h2_ref[...], wup_ref[:, lo:lo + ck])


def _ffn_conv(u, off, ck, part, cw_ref, cb_ref, tail_ref, d_ff):
    taps = CONV_W - 1
    lo = part * d_ff + off
    u5 = _blocks(u)
    carry = tail_ref[:, lo:lo + ck].reshape(taps, SUBLANES, ck)
    halo = _wrapped(u5, carry, taps)
    tail_ref[:, lo:lo + ck] = u5[u5.shape[0] - 1, RUN - taps:RUN].reshape(taps * SUBLANES, ck)
    w = cw_ref[:, lo:lo + ck]
    return (_shifted(u5, halo, 2) * w[0:1] + _shifted(u5, halo, 1) * w[1:2] + u5 * w[2:3]
            + cb_ref[:, lo:lo + ck])


def _pool_mixer(h1, carry_ref, poolw_ref, tile_pos):
    h5 = _blocks(h1)
    nb = h5.shape[0]
    gc = h1.shape[1] // len(POOL_WINDOWS)
    carry5 = carry_ref[...].reshape(RUN, SUBLANES, h1.shape[1])
    shape = (nb, RUN, SUBLANES, 1)
    tpos = (tile_pos + BLOCK * lax.broadcasted_iota(jnp.int32, shape, 0)
            + RUN * lax.broadcasted_iota(jnp.int32, shape, 2)
            + lax.broadcasted_iota(jnp.int32, shape, 1))
    ys = []
    for gi, win in enumerate(POOL_WINDOWS):
        cols = slice(gi * gc, (gi + 1) * gc)
        hg = h5[:, :, :, cols]
        halo = _wrapped(hg, carry5[RUN - (win - 1):RUN, :, cols], win - 1)
        s = jnp.concatenate([halo, hg], axis=1)
        step = 1
        while step < win:
            n = s.shape[1]
            s = s[:, step:n] + s[:, 0:n - step]
            step *= 2
        inv_cnt = 1.0 / jnp.minimum(tpos + 1, win).astype(F32)
        pooled = s * inv_cnt - hg
        ys.append(_dot(_unblocks(pooled).astype(BF16), poolw_ref[gi]))
        if gi % 2 == 1:
            yield ys
            ys = []
    carry_ref[...] = h1[h1.shape[0] - BLOCK:h1.shape[0]]


def _attention_bias():
    shape = (BLOCK, 2 * BLOCK)
    qrow = lax.broadcasted_iota(jnp.int32, shape, 0)
    kcol = lax.broadcasted_iota(jnp.int32, shape, 1)
    qpos = RUN * (qrow & (SUBLANES - 1)) + (qrow >> 3)
    kin = kcol & (BLOCK - 1)
    kpos = RUN * (kin & (SUBLANES - 1)) + (kin >> 3) + jnp.where(kcol >= BLOCK, 0, -BLOCK)
    ok = (kpos > qpos - WINDOW) & (kpos <= qpos)
    return jnp.where(ok, 0.0, NEG_BIG).astype(F32), kcol


def _swa_attention(q, kvc_ref, kvp_ref, sinks_ref, o_ref, is_first):
    tm = q.shape[0]
    n_pairs = q.shape[1] // LANES
    kdim = N_KV * HEAD_DIM
    lane = lax.broadcasted_iota(jnp.int32, (1, LANES), 1)
    first_half = lane < HEAD_DIM
    bias, kcol = _attention_bias()
    first_key = jnp.where(is_first, BLOCK, 0)
    bias_first = jnp.where(kcol < first_key, NEG_BIG, bias)
    zero = jnp.zeros((), BF16)
    ind_a = jnp.broadcast_to(jnp.where(first_half, 1.0, 0.0), (2 * BLOCK, LANES)).astype(BF16)
    ind_b = jnp.broadcast_to(jnp.where(first_half, 0.0, 1.0), (2 * BLOCK, LANES)).astype(BF16)
    contract_last = (((1,), (1,)), ((), ()))
    for nb in range(tm // BLOCK):
        rows = slice(nb * BLOCK, (nb + 1) * BLOCK)
        prev = kvp_ref[...] if nb == 0 else kvc_ref[(nb - 1) * BLOCK:nb * BLOCK, :]
        blk_bias = bias_first if nb == 0 else bias
        kvw = jnp.concatenate([prev, kvc_ref[rows, :]], axis=0)
        kw, vw = kvw[:, 0:kdim], kvw[:, kdim:2 * kdim]
        k_a, k_b = jnp.where(first_half, kw, zero), jnp.where(first_half, zero, kw)
        v_a = jnp.concatenate([jnp.where(first_half, vw, zero), ind_a], axis=1)
        v_b = jnp.concatenate([jnp.where(first_half, zero, vw), ind_b], axis=1)
        qst = jnp.concatenate(
            [q[rows, j * LANES:(j + 1) * LANES] for j in range(n_pairs)], axis=0)
        probs, sink_terms = [], []
        for k_x, head0 in ((k_a, 0), (k_b, n_pairs)):
            s = lax.dot_general(qst, k_x, contract_last, preferred_element_type=F32)
            ps, ts = [], []
            for j in range(n_pairs):
                sink = sinks_ref[head0 + j]
                sj = s[j * BLOCK:(j + 1) * BLOCK] + blk_bias
                m = jnp.maximum(jnp.max(sj, axis=-1, keepdims=True), sink)
                ps.append(jnp.exp(sj - m).astype(BF16))
                ts.append(jnp.exp(sink - m))
            probs.append(jnp.concatenate(ps, axis=0))
            sink_terms.append(ts)
        o_ext = _dot(probs[0], v_a) + _dot(probs[1], v_b)
        for j in range(n_pairs):
            pr = slice(j * BLOCK, (j + 1) * BLOCK)
            denom = o_ext[pr, LANES:] + jnp.where(first_half, sink_terms[0][j], sink_terms[1][j])
            o_ref[rows, j * LANES:(j + 1) * LANES] = (o_ext[pr, :LANES] / denom).astype(BF16)
        yield


def _layer_kernel(*refs, mixer, emit_kv, final_norm, permute_in, permute_out, tm, d_ff,
                  tiles_per_seq):
    refs = list(refs)
    x_ref, g1_ref, g2_ref = refs[0:3]
    pos = 3
    if mixer == "pool":
        poolw_ref, pscale_ref = refs[pos:pos + 2]
        pos += 2
    else:
        kvc_ref, kvp_ref, wq_ref, bq_ref, sinks_ref, wo_ref, bo_ref = refs[pos:pos + 7]
        pos += 7
    wup_ref, cw_ref, cb_ref, wdn_ref = refs[pos:pos + 4]
    pos += 4
    if emit_kv:
        kvg_ref, wkv_ref, bkv_ref = refs[pos:pos + 3]
        pos += 3
    if final_norm:
        fg_ref = refs[pos]
        pos += 1
    xo_ref = refs[pos]
    pos += 1
    if emit_kv:
        kvo_ref = refs[pos]
        pos += 1
    xmid_ref, h2_ref, xcur_ref, hcur_ref, tail_ref, mix_ref = refs[pos:pos + 6]
    pos += 6
    if permute_in or permute_out:
        slab_ref = refs[pos]

    i = pl.program_id(0)
    n_tiles = pl.num_programs(0) - 1
    seq_tile_mix = jnp.minimum(i, n_tiles - 1) % tiles_per_seq
    seq_tile_ffn = jnp.maximum(i - 1, 0) % tiles_per_seq

    @pl.when(i == 0)
    def _():
        xmid_ref[...] = jnp.zeros_like(xmid_ref)
        h2_ref[...] = jnp.zeros_like(h2_ref)

    @pl.when(seq_tile_ffn == 0)
    def _():
        tail_ref[...] = jnp.zeros_like(tail_ref)

    if mixer == "pool":
        @pl.when(seq_tile_mix == 0)
        def _():
            mix_ref[...] = jnp.zeros_like(mix_ref)

    xcur_ref[...] = xmid_ref[...]
    hcur_ref[...] = h2_ref[...]

    def token_mixer():
        x = _load_permuted(x_ref, slab_ref) if permute_in else x_ref[...]
        h1 = _rmsnorm(x, g1_ref[...])
        if mixer == "pool":
            stages = _pool_mixer(h1, mix_ref, poolw_ref, seq_tile_mix * tm)
            ys = []
            for part in stages:
                ys.extend(part)
                yield
            x = x + jnp.concatenate(ys, axis=1) * pscale_ref[...]
        else:
            q = (_dot(h1.astype(BF16), wq_ref[...]) + bq_ref[...]) * (HEAD_DIM ** -0.5)
            yield
            yield from _swa_attention(q.astype(BF16), kvc_ref, kvp_ref, sinks_ref, mix_ref,
                                      seq_tile_mix == 0)
            x = x + _dot(mix_ref[...], wo_ref[...]) + bo_ref[...]
        xmid_ref[...] = x
        h2_ref[...] = _rmsnorm(x, g2_ref[...]).astype(BF16)

    chunks = _ff_chunks(d_ff)
    mix_stages = token_mixer()
    up = lambda c, part: _ffn_up(*chunks[c], part, hcur_ref, wup_ref, d_ff)
    conv = lambda u, c, part: _ffn_conv(u, *chunks[c], part, cw_ref, cb_ref, tail_ref, d_ff)
    down = lambda act, c: _dot(act, wdn_ref[chunks[c][0]:chunks[c][0] + chunks[c][1], :])
    n_chunks = len(chunks)
    y = xcur_ref[...]
    u_gate = up(0, 0)
    next(mix_stages, None)
    u_val = up(0, 1)
    act_prev = None
    for c in range(n_chunks):
        more = c + 1 < n_chunks
        u_gate_next = up(c + 1, 0) if more else None
        gate = conv(u_gate, c, 0)
        u_val_next = up(c + 1, 1) if more else None
        val = conv(u_val, c, 1)
        act = _unblocks(gate / (1.0 + jnp.exp(-gate)) * val).astype(BF16)
        if act_prev is not None:
            y = y + down(act_prev, c - 1)
        next(mix_stages, None)
        act_prev, u_gate, u_val = act, u_gate_next, u_val_next
    y = y + down(act_prev, n_chunks - 1)
    for _ in mix_stages:
        pass
    if emit_kv:
        kvn = _rmsnorm(y, kvg_ref[...]).astype(BF16)
        kvo_ref[...] = (_dot(kvn, wkv_ref[...]) + bkv_ref[...]).astype(BF16)
    if final_norm:
        y = _rmsnorm(y, fg_ref[...])
    if permute_out:
        _store_unpermuted(y, xo_ref, slab_ref)
    else:
        xo_ref[...] = y


def _resident(shape, layer=None):
    if layer is None:
        return pl.BlockSpec(shape, lambda i: (0,) * len(shape), pipeline_mode=pl.Buffered(1))
    return pl.BlockSpec((None,) + shape, lambda i: (layer,) + (0,) * len(shape),
                        pipeline_mode=pl.Buffered(1))


def _run_layer(x, layer, p, *, mixer, seq_len, kv=None, emit_kv=False, final_norm=False,
               permute_in=False, permute_out=False):
    n_tok, d = x.shape
    d_ff = p["ffn_down"].shape[1]
    tm = TOKEN_TILE
    assert seq_len % tm == 0 and n_tok % seq_len == 0 and tm % BLOCK == 0
    n_tiles = n_tok // tm
    row = lambda n: _resident((1, n), layer)
    tok_in = lambda n: pl.BlockSpec((tm, n), lambda i: (jnp.minimum(i, n_tiles - 1), 0))
    tok_out = lambda n: pl.BlockSpec((tm, n), lambda i: (jnp.maximum(i - 1, 0), 0))

    args = [x, p["norm1_g"], p["norm2_g"]]
    specs = [tok_in(d), row(d), row(d)]
    if mixer == "pool":
        gc = d // len(POOL_WINDOWS)
        args += [p["pool_w"], p["pool_scale"]]
        specs += [_resident((len(POOL_WINDOWS), gc, gc), layer), row(d)]
        mix_scratch = pltpu.VMEM((BLOCK, d), F32)
    else:
        j = layer - p["n_pool"]
        blocks_per_tile = tm // BLOCK
        kvw = kv.shape[1]
        prev_spec = pl.BlockSpec(
            (BLOCK, kvw),
            lambda i: (jnp.maximum(jnp.minimum(i, n_tiles - 1) * blocks_per_tile - 1, 0), 0))
        args += [kv, kv, p["w_q"], p["b_q"], p["sinks"][j], p["w_o"], p["b_o"]]
        specs += [tok_in(kvw), prev_spec, _resident((d, d), j), _resident((1, d), j),
                  pl.BlockSpec(memory_space=pltpu.SMEM),
                  _resident((d, d), j), _resident((1, d), j)]
        mix_scratch = pltpu.VMEM((tm, d), BF16)
    args += [p["ffn_up"], p["ffn_conv_w"], p["ffn_conv_b"], p["ffn_down"]]
    specs += [_resident((d, 2 * d_ff), layer), _resident((CONV_W, 2 * d_ff), layer),
              row(2 * d_ff), _resident((d_ff, d), layer)]
    if emit_kv:
        kvw = p["w_kv"].shape[1]
        args += [p["kv_norm_g"], p["w_kv"], p["b_kv"]]
        specs += [_resident((1, d)), _resident((d, kvw)), _resident((1, kvw))]
    if final_norm:
        args += [p["final_g"]]
        specs += [_resident((1, d))]

    out_shape = [jax.ShapeDtypeStruct((n_tok, d), F32)]
    out_specs = [tok_out(d)]
    if emit_kv:
        out_shape.append(jax.ShapeDtypeStruct((n_tok, kvw), BF16))
        out_specs.append(tok_out(kvw))

    scratch = [pltpu.VMEM((tm, d), F32),
               pltpu.VMEM((tm, d), BF16),
               pltpu.VMEM((tm, d), F32),
               pltpu.VMEM((tm, d), BF16),
               pltpu.VMEM(((CONV_W - 1) * SUBLANES, 2 * d_ff), F32),
               mix_scratch]
    if permute_in or permute_out:
        scratch.append(pltpu.VMEM((d // LANES, tm, LANES), F32))

    kern = functools.partial(
        _layer_kernel, mixer=mixer, emit_kv=emit_kv, final_norm=final_norm,
        permute_in=permute_in, permute_out=permute_out, tm=tm, d_ff=d_ff,
        tiles_per_seq=seq_len // tm)
    out = pl.pallas_call(
        kern,
        grid=(n_tiles + 1,),
        in_specs=specs,
        out_specs=out_specs,
        out_shape=out_shape,
        scratch_shapes=scratch,
        compiler_params=pltpu.CompilerParams(
            dimension_semantics=("arbitrary",), vmem_limit_bytes=VMEM_LIMIT_BYTES),
        name=f"layer{layer}_{mixer}",
    )(*args)
    return out


def kernel(x, norm1_g, norm2_g, pool_w, pool_scale, kv_norm_g, w_kv, b_kv, w_q, b_q, sinks,
           w_o, b_o, ffn_up, ffn_conv_w, ffn_conv_b, ffn_down, final_g):
    batch, seq_len, d = x.shape
    depth = norm1_g.shape[0]
    n_pool = pool_w.shape[0]
    n_heads = d // HEAD_DIM
    half = n_heads // 2
    perm = np.concatenate([
        np.concatenate([np.arange(j * HEAD_DIM, (j + 1) * HEAD_DIM),
                        np.arange((half + j) * HEAD_DIM, (half + j + 1) * HEAD_DIM)])
        for j in range(half)])
    p = {
        "n_pool": n_pool,
        "norm1_g": norm1_g.reshape(depth, 1, d), "norm2_g": norm2_g.reshape(depth, 1, d),
        "pool_w": pool_w.astype(BF16), "pool_scale": pool_scale.reshape(n_pool, 1, d),
        "kv_norm_g": kv_norm_g.reshape(1, d), "w_kv": w_kv.astype(BF16),
        "b_kv": b_kv.reshape(1, -1),
        "w_q": w_q[:, :, perm].astype(BF16), "b_q": b_q[:, perm].reshape(-1, 1, d),
        "sinks": sinks, "w_o": w_o[:, perm, :].astype(BF16), "b_o": b_o.reshape(-1, 1, d),
        "ffn_up": ffn_up.astype(BF16), "ffn_conv_w": ffn_conv_w,
        "ffn_conv_b": ffn_conv_b.reshape(depth, 1, -1), "ffn_down": ffn_down.astype(BF16),
        "final_g": final_g.reshape(1, d),
    }
    xt = x.reshape(batch * seq_len, d)
    kv = None
    for layer in range(depth):
        last_pool = layer == n_pool - 1
        common = dict(seq_len=seq_len, permute_in=layer == 0, permute_out=layer == depth - 1)
        if layer < n_pool:
            out = _run_layer(xt, layer, p, mixer="pool", emit_kv=last_pool, **common)
        else:
            out = _run_layer(xt, layer, p, mixer="attn", kv=kv,
                             final_norm=layer == depth - 1, **common)
        xt = out[0]
        if last_pool:
            kv = out[1]
    return xt.reshape(batch, seq_len, d)
```

```python
import functools

import jax
import jax.numpy as jnp
from jax import lax
from jax.experimental import pallas as pl
from jax.experimental.pallas import tpu as pltpu

F32 = jnp.float32
BF16 = jnp.bfloat16

POOL_WINDOWS = (2, 4, 8, 16)
HEAD_DIM = 64
N_KV = 2
WINDOW = 128
CONV_W = 3
EPS = 1e-5
NEG_BIG = -1e30

SUBLANES = 8
LANES = 128
BLOCK = WINDOW
RUN = BLOCK // SUBLANES
VMEM_LIMIT_BYTES = 56 * 1024 * 1024

TOKEN_TILE = 512
FF_CHUNK = 512


def _ff_chunks(d_ff):
    chunks, off = [], 0
    while off < d_ff:
        ck = min(FF_CHUNK, d_ff - off)
        chunks.append((off, ck))
        off += ck
    return tuple(chunks)


def _rmsnorm(x, g):
    ms = jnp.mean(x * x, axis=-1, keepdims=True)
    return x * lax.rsqrt(ms + EPS) * g


def _dot(a, b):
    return jnp.dot(a, b, preferred_element_type=F32)


def _blocks(a):
    return a.reshape(a.shape[0] // BLOCK, RUN, SUBLANES, a.shape[1])


def _unblocks(a5):
    return a5.reshape(a5.shape[0] * BLOCK, a5.shape[3])


def _wrapped(a5, carry, k):
    last = a5[:, RUN - k:RUN]
    prev_last = jnp.concatenate([carry[None], last[:-1]], axis=0)
    sub = lax.broadcasted_iota(jnp.int32, (1, 1, SUBLANES, 1), 2)
    return pltpu.roll(jnp.where(sub == SUBLANES - 1, prev_last, last), 1, axis=2)


def _shifted(a5, halo, j):
    k = halo.shape[1]
    return jnp.concatenate([halo[:, k - j:k], a5[:, 0:RUN - j]], axis=1)


def _load_permuted(x_ref, slab_ref):
    tm, d = x_ref.shape
    for c in range(d // LANES):
        slab_ref[c] = x_ref[:, c * LANES:(c + 1) * LANES]
    groups = []
    for n in range(tm // BLOCK):
        for r in range(RUN):
            groups.append(jnp.concatenate(
                [slab_ref[c, pl.ds(BLOCK * n + r, SUBLANES, stride=RUN), :]
                 for c in range(d // LANES)], axis=1))
    return jnp.concatenate(groups, axis=0)


def _store_unpermuted(y, o_ref, slab_ref):
    tm, d = o_ref.shape
    for n in range(tm // BLOCK):
        for r in range(RUN):
            row0 = (n * RUN + r) * SUBLANES
            for c in range(d // LANES):
                slab_ref[c, pl.ds(BLOCK * n + r, SUBLANES, stride=RUN), :] = (
                    y[row0:row0 + SUBLANES, c * LANES:(c + 1) * LANES])
    for c in range(d // LANES):
        o_ref[:, c * LANES:(c + 1) * LANES] = slab_ref[c]


def _ffn_up(off, ck, part, h2_ref, wup_ref, d_ff):
    lo = part * d_ff + off
    return _dot(h2_ref[...], wup_ref[:, lo:lo + ck])


def _ffn_conv(u, off, ck, part, cw_ref, cb_ref, tail_ref, d_ff):
    taps = CONV_W - 1
    lo = part * d_ff + off
    u5 = _blocks(u)
    carry = tail_ref[:, lo:lo + ck].reshape(taps, SUBLANES, ck)
    halo = _wrapped(u5, carry, taps)
    tail_ref[:, lo:lo + ck] = u5[u5.shape[0] - 1, RUN - taps:RUN].reshape(taps * SUBLANES, ck)
    w = cw_ref[:, lo:lo + ck]
    return (_shifted(u5, halo, 2) * w[0:1] + _shifted(u5, halo, 1) * w[1:2] + u5 * w[2:3]
            + cb_ref[:, lo:lo + ck])


def _pool_mixer(h1, carry_ref, poolw_ref, tile_pos):
    h5 = _blocks(h1)
    nb = h5.shape[0]
    gc = h1.shape[1] // len(POOL_WINDOWS)
    carry5 = carry_ref[...].reshape(RUN, SUBLANES, h1.shape[1])
    shape = (nb, RUN, SUBLANES, 1)
    tpos = (tile_pos + BLOCK * lax.broadcasted_iota(jnp.int32, shape, 0)
            + RUN * lax.broadcasted_iota(jnp.int32, shape, 2)
            + lax.broadcasted_iota(jnp.int32, shape, 1))
    ys = []
    for gi, win in enumerate(POOL_WINDOWS):
        cols = slice(gi * gc, (gi + 1) * gc)
        hg = h5[:, :, :, cols]
        halo = _wrapped(hg, carry5[RUN - (win - 1):RUN, :, cols], win - 1)
        s = jnp.concatenate([halo, hg], axis=1)
        step = 1
        while step < win:
            n = s.shape[1]
            s = s[:, step:n] + s[:, 0:n - step]
            step *= 2
        inv_cnt = 1.0 / jnp.minimum(tpos + 1, win).astype(F32)
        pooled = s * inv_cnt - hg
        ys.append(_dot(_unblocks(pooled).astype(BF16), poolw_ref[gi]))
        if gi % 2 == 1:
            yield ys
            ys = []
    carry_ref[...] = h1[h1.shape[0] - BLOCK:h1.shape[0]]


def _attention_bias():
    shape = (BLOCK, 2 * BLOCK)
    qrow = lax.broadcasted_iota(jnp.int32, shape, 0)
    kcol = lax.broadcasted_iota(jnp.int32, shape, 1)
    qpos = RUN * (qrow & (SUBLANES - 1)) + (qrow >> 3)
    kin = kcol & (BLOCK - 1)
    kpos = RUN * (kin & (SUBLANES - 1)) + (kin >> 3) + jnp.where(kcol >= BLOCK, 0, -BLOCK)
    ok = (kpos > qpos - WINDOW) & (kpos <= qpos)
    return jnp.where(ok, 0.0, NEG_BIG).astype(F32), kcol


def _swa_attention(q, kvc_ref, kvp_ref, sinks_ref, o_ref, is_first):
    tm = q.shape[0]
    n_pairs = q.shape[1] // LANES
    pairs_per_kv = n_pairs // N_KV
    lane = lax.broadcasted_iota(jnp.int32, (1, LANES), 1)
    first_half = lane < HEAD_DIM
    bias, kcol = _attention_bias()
    first_key = jnp.where(is_first, BLOCK, 0)
    bias_first = jnp.where(kcol < first_key, NEG_BIG, bias)
    zero = jnp.zeros((), BF16)
    ind_a = jnp.broadcast_to(jnp.where(first_half, 1.0, 0.0), (2 * BLOCK, LANES)).astype(BF16)
    ind_b = jnp.broadcast_to(jnp.where(first_half, 0.0, 1.0), (2 * BLOCK, LANES)).astype(BF16)
    contract_last = (((1,), (1,)), ((), ()))
    for nb in range(tm // BLOCK):
        rows = slice(nb * BLOCK, (nb + 1) * BLOCK)
        prev = kvp_ref[...] if nb == 0 else kvc_ref[(nb - 1) * BLOCK:nb * BLOCK, :]
        blk_bias = bias_first if nb == 0 else bias
        kvw = jnp.concatenate([prev, kvc_ref[rows, :]], axis=0)
        for g in range(N_KV):
            kw = kvw[:, g * LANES:(g + 1) * LANES]
            vw = kvw[:, (N_KV + g) * LANES:(N_KV + g + 1) * LANES]
            k_a, k_b = jnp.where(first_half, kw, zero), jnp.where(first_half, zero, kw)
            v_a = jnp.concatenate([jnp.where(first_half, vw, zero), ind_a], axis=1)
            v_b = jnp.concatenate([jnp.where(first_half, zero, vw), ind_b], axis=1)
            pair0 = g * pairs_per_kv
            qst = jnp.concatenate(
                [q[rows, (pair0 + j) * LANES:(pair0 + j + 1) * LANES]
                 for j in range(pairs_per_kv)], axis=0)
            probs, sink_terms = [], []
            for k_x, odd in ((k_a, 0), (k_b, 1)):
                s = lax.dot_general(qst, k_x, contract_last, preferred_element_type=F32)
                ps, ts = [], []
                for j in range(pairs_per_kv):
                    sink = sinks_ref[2 * (pair0 + j) + odd]
                    sj = s[j * BLOCK:(j + 1) * BLOCK] + blk_bias
                    m = jnp.maximum(jnp.max(sj, axis=-1, keepdims=True), sink)
                    ps.append(jnp.exp(sj - m).astype(BF16))
                    ts.append(jnp.exp(sink - m))
                probs.append(jnp.concatenate(ps, axis=0))
                sink_terms.append(ts)
            o_ext = _dot(probs[0], v_a) + _dot(probs[1], v_b)
            for j in range(pairs_per_kv):
                pr = slice(j * BLOCK, (j + 1) * BLOCK)
                denom = o_ext[pr, LANES:] + jnp.where(
                    first_half, sink_terms[0][j], sink_terms[1][j])
                o_ref[rows, (pair0 + j) * LANES:(pair0 + j + 1) * LANES] = (
                    o_ext[pr, :LANES] / denom).astype(BF16)
        yield


def _layer_kernel(*refs, mixer, emit_kv, final_norm, permute_in, permute_out, tm, d_ff,
                  tiles_per_seq):
    refs = list(refs)
    x_ref, g1_ref, g2_ref = refs[0:3]
    pos = 3
    if mixer == "pool":
        poolw_ref, pscale_ref = refs[pos:pos + 2]
        pos += 2
    else:
        kvc_ref, kvp_ref, wq_ref, bq_ref, sinks_ref, wo_ref, bo_ref = refs[pos:pos + 7]
        pos += 7
    wup_ref, cw_ref, cb_ref, wdn_ref = refs[pos:pos + 4]
    pos += 4
    if emit_kv:
        kvg_ref, wkv_ref, bkv_ref = refs[pos:pos + 3]
        pos += 3
    if final_norm:
        fg_ref = refs[pos]
        pos += 1
    xo_ref = refs[pos]
    pos += 1
    if emit_kv:
        kvo_ref = refs[pos]
        pos += 1
    xmid_ref, h2_ref, xcur_ref, hcur_ref, tail_ref, mix_ref = refs[pos:pos + 6]
    pos += 6
    if permute_in or permute_out:
        slab_ref = refs[pos]

    i = pl.program_id(0)
    n_tiles = pl.num_programs(0) - 1
    seq_tile_mix = jnp.minimum(i, n_tiles - 1) % tiles_per_seq
    seq_tile_ffn = jnp.maximum(i - 1, 0) % tiles_per_seq

    @pl.when(i == 0)
    def _():
        xmid_ref[...] = jnp.zeros_like(xmid_ref)
        h2_ref[...] = jnp.zeros_like(h2_ref)

    @pl.when(seq_tile_ffn == 0)
    def _():
        tail_ref[...] = jnp.zeros_like(tail_ref)

    if mixer == "pool":
        @pl.when(seq_tile_mix == 0)
        def _():
            mix_ref[...] = jnp.zeros_like(mix_ref)

    xcur_ref[...] = xmid_ref[...]
    hcur_ref[...] = h2_ref[...]

    def token_mixer():
        x = _load_permuted(x_ref, slab_ref) if permute_in else x_ref[...]
        h1 = _rmsnorm(x, g1_ref[...])
        if mixer == "pool":
            stages = _pool_mixer(h1, mix_ref, poolw_ref, seq_tile_mix * tm)
            ys = []
            for part in stages:
                ys.extend(part)
                yield
            x = x + jnp.concatenate(ys, axis=1) * pscale_ref[...]
        else:
            q = (_dot(h1.astype(BF16), wq_ref[...]) + bq_ref[...]) * (HEAD_DIM ** -0.5)
            yield
            yield from _swa_attention(q.astype(BF16), kvc_ref, kvp_ref, sinks_ref, mix_ref,
                                      seq_tile_mix == 0)
            x = x + _dot(mix_ref[...], wo_ref[...]) + bo_ref[...]
        xmid_ref[...] = x
        h2_ref[...] = _rmsnorm(x, g2_ref[...]).astype(BF16)

    chunks = _ff_chunks(d_ff)
    mix_stages = token_mixer()
    up = lambda c, part: _ffn_up(*chunks[c], part, hcur_ref, wup_ref, d_ff)
    conv = lambda u, c, part: _ffn_conv(u, *chunks[c], part, cw_ref, cb_ref, tail_ref, d_ff)
    down = lambda act, c: _dot(act, wdn_ref[chunks[c][0]:chunks[c][0] + chunks[c][1], :])
    n_chunks = len(chunks)
    y = xcur_ref[...]
    u_gate = up(0, 0)
    next(mix_stages, None)
    u_val = up(0, 1)
    act_prev = None
    for c in range(n_chunks):
        more = c + 1 < n_chunks
        u_gate_next = up(c + 1, 0) if more else None
        gate = conv(u_gate, c, 0)
        u_val_next = up(c + 1, 1) if more else None
        val = conv(u_val, c, 1)
        act = _unblocks(gate / (1.0 + jnp.exp(-gate)) * val).astype(BF16)
        if act_prev is not None:
            y = y + down(act_prev, c - 1)
        next(mix_stages, None)
        act_prev, u_gate, u_val = act, u_gate_next, u_val_next
    y = y + down(act_prev, n_chunks - 1)
    for _ in mix_stages:
        pass
    if emit_kv:
        kvn = _rmsnorm(y, kvg_ref[...]).astype(BF16)
        kvo_ref[...] = (_dot(kvn, wkv_ref[...]) + bkv_ref[...]).astype(BF16)
    if final_norm:
        y = _rmsnorm(y, fg_ref[...])
    if permute_out:
        _store_unpermuted(y, xo_ref, slab_ref)
    else:
        xo_ref[...] = y


def _resident(shape, layer=None):
    if layer is None:
        return pl.BlockSpec(shape, lambda i: (0,) * len(shape), pipeline_mode=pl.Buffered(1))
    return pl.BlockSpec((None,) + shape, lambda i: (layer,) + (0,) * len(shape),
                        pipeline_mode=pl.Buffered(1))


def _run_layer(x, layer, p, *, mixer, seq_len, kv=None, emit_kv=False, final_norm=False,
               permute_in=False, permute_out=False):
    n_tok, d = x.shape
    d_ff = p["ffn_down"].shape[1]
    tm = TOKEN_TILE
    assert seq_len % tm == 0 and n_tok % seq_len == 0 and tm % BLOCK == 0
    n_tiles = n_tok // tm
    row = lambda n: _resident((1, n), layer)
    tok_in = lambda n: pl.BlockSpec((tm, n), lambda i: (jnp.minimum(i, n_tiles - 1), 0))
    tok_out = lambda n: pl.BlockSpec((tm, n), lambda i: (jnp.maximum(i - 1, 0), 0))

    args = [x, p["norm1_g"], p["norm2_g"]]
    specs = [tok_in(d), row(d), row(d)]
    if mixer == "pool":
        gc = d // len(POOL_WINDOWS)
        args += [p["pool_w"], p["pool_scale"]]
        specs += [_resident((len(POOL_WINDOWS), gc, gc), layer), row(d)]
        mix_scratch = pltpu.VMEM((BLOCK, d), F32)
    else:
        j = layer - p["n_pool"]
        blocks_per_tile = tm // BLOCK
        kvw = kv.shape[1]
        prev_spec = pl.BlockSpec(
            (BLOCK, kvw),
            lambda i: (jnp.maximum(jnp.minimum(i, n_tiles - 1) * blocks_per_tile - 1, 0), 0))
        args += [kv, kv, p["w_q"], p["b_q"], p["sinks"][j], p["w_o"], p["b_o"]]
        specs += [tok_in(kvw), prev_spec, _resident((d, d), j), _resident((1, d), j),
                  pl.BlockSpec(memory_space=pltpu.SMEM),
                  _resident((d, d), j), _resident((1, d), j)]
        mix_scratch = pltpu.VMEM((tm, d), BF16)
    args += [p["ffn_up"], p["ffn_conv_w"], p["ffn_conv_b"], p["ffn_down"]]
    specs += [_resident((d, 2 * d_ff), layer), _resident((CONV_W, 2 * d_ff), layer),
              row(2 * d_ff), _resident((d_ff, d), layer)]
    if emit_kv:
        kvw = p["w_kv"].shape[1]
        args += [p["kv_norm_g"], p["w_kv"], p["b_kv"]]
        specs += [_resident((1, d)), _resident((d, kvw)), _resident((1, kvw))]
    if final_norm:
        args += [p["final_g"]]
        specs += [_resident((1, d))]

    out_shape = [jax.ShapeDtypeStruct((n_tok, d), F32)]
    out_specs = [tok_out(d)]
    if emit_kv:
        out_shape.append(jax.ShapeDtypeStruct((n_tok, kvw), BF16))
        out_specs.append(tok_out(kvw))

    scratch = [pltpu.VMEM((tm, d), F32),
               pltpu.VMEM((tm, d), BF16),
               pltpu.VMEM((tm, d), F32),
               pltpu.VMEM((tm, d), BF16),
               pltpu.VMEM(((CONV_W - 1) * SUBLANES, 2 * d_ff), F32),
               mix_scratch]
    if permute_in or permute_out:
        scratch.append(pltpu.VMEM((d // LANES, tm, LANES), F32))

    kern = functools.partial(
        _layer_kernel, mixer=mixer, emit_kv=emit_kv, final_norm=final_norm,
        permute_in=permute_in, permute_out=permute_out, tm=tm, d_ff=d_ff,
        tiles_per_seq=seq_len // tm)
    out = pl.pallas_call(
        kern,
        grid=(n_tiles + 1,),
        in_specs=specs,
        out_specs=out_specs,
        out_shape=out_shape,
        scratch_shapes=scratch,
        compiler_params=pltpu.CompilerParams(
            dimension_semantics=("arbitrary",), vmem_limit_bytes=VMEM_LIMIT_BYTES),
        name=f"layer{layer}_{mixer}",
    )(*args)
    return out


def kernel(x, norm1_g, norm2_g, pool_w, pool_scale, kv_norm_g, w_kv, b_kv, w_q, b_q, sinks,
           w_o, b_o, ffn_up, ffn_conv_w, ffn_conv_b, ffn_down, final_g):
    batch, seq_len, d = x.shape
    depth = norm1_g.shape[0]
    n_pool = pool_w.shape[0]
    dup_heads = lambda a: jnp.repeat(
        a.reshape(a.shape[0], 2 * N_KV, 1, HEAD_DIM), 2, axis=2).reshape(a.shape[0], -1)
    p = {
        "n_pool": n_pool,
        "norm1_g": norm1_g.reshape(depth, 1, d), "norm2_g": norm2_g.reshape(depth, 1, d),
        "pool_w": pool_w.astype(BF16), "pool_scale": pool_scale.reshape(n_pool, 1, d),
        "kv_norm_g": kv_norm_g.reshape(1, d), "w_kv": dup_heads(w_kv).astype(BF16),
        "b_kv": dup_heads(b_kv.reshape(1, -1)),
        "w_q": w_q.astype(BF16), "b_q": b_q.reshape(-1, 1, d),
        "sinks": sinks, "w_o": w_o.astype(BF16), "b_o": b_o.reshape(-1, 1, d),
        "ffn_up": ffn_up.astype(BF16), "ffn_conv_w": ffn_conv_w,
        "ffn_conv_b": ffn_conv_b.reshape(depth, 1, -1), "ffn_down": ffn_down.astype(BF16),
        "final_g": final_g.reshape(1, d),
    }
    xt = x.reshape(batch * seq_len, d)
    kv = None
    for layer in range(depth):
        last_pool = layer == n_pool - 1
        common = dict(seq_len=seq_len, permute_in=layer == 0, permute_out=layer == depth - 1)
        if layer < n_pool:
            out = _run_layer(xt, layer, p, mixer="pool", emit_kv=last_pool, **common)
        else:
            out = _run_layer(xt, layer, p, mixer="attn", kv=kv,
                             final_norm=layer == depth - 1, **common)
        xt = out[0]
        if last_pool:
            kv = out[1]
    return xt.reshape(batch, seq_len, d)
```
